```python
import jax
import jax.numpy as jnp
from jax import lax
import numpy as np

D_MODEL = 2048
BATCH = 4
SEQ = 4096
DEPTH = 1

CTX_LEN = 256
GRID_W = 64
MIX_WIDTH = D_MODEL
MLSTM_WIDTH = MIX_WIDTH // 2
MLSTM_HEADS = 4
MLSTM_HEAD_DIM = MLSTM_WIDTH // MLSTM_HEADS
MLSTM_CHUNK = 128
CONV_WIDTH = MIX_WIDTH - MLSTM_WIDTH
CONV_KERNEL = 31
N_EXPERTS = 16
EC_CAPACITY_FACTOR = 2
EXPERT_FF = 5504
N_ADA = 6
IN_COLS = 4 * MLSTM_WIDTH + 4 * MLSTM_HEADS + 2 * CONV_WIDTH
EPS = 1e-6
F_BIAS_LO = 3.0
F_BIAS_HI = 6.0

kernel_name = 'hybrid_mlstm_conformer_ecmoe_dit_block'


def _rmsnorm(x, w):
    xf = x.astype(jnp.float32)
    xf = xf * lax.rsqrt(jnp.mean(xf * xf, axis=-1, keepdims=True) + EPS)
    return xf.astype(x.dtype) * w


def _layernorm_f32(x):
    xf = x.astype(jnp.float32)
    mu = jnp.mean(xf, axis=-1, keepdims=True)
    var = jnp.mean(jnp.square(xf - mu), axis=-1, keepdims=True)
    return (xf - mu) * lax.rsqrt(var + EPS)


def _ada_params(cvec, w_ada, b_ada):
    return jnp.split(jax.nn.silu(cvec) @ w_ada + b_ada, N_ADA, axis=-1)


def _split_projection(p):
    W, H = MLSTM_WIDTH, MLSTM_HEADS
    return jnp.split(p, [W, 2 * W, 3 * W, 4 * W, 4 * W + 4 * H], axis=-1)


def _mlstm_inputs(q, k, v, gates, b_i, b_f):
    B, S, _ = q.shape

    def heads(t):
        return t.astype(jnp.float32).reshape(B, S, MLSTM_HEADS, MLSTM_HEAD_DIM).transpose(0, 2, 1, 3)

    g = gates.astype(jnp.float32).reshape(B, S, 2, 2, MLSTM_HEADS) + jnp.stack([b_i, b_f], axis=1).astype(jnp.float32)
    g = g.transpose(2, 3, 0, 4, 1)
    ig = g[:, 0]
    lf = jax.nn.log_sigmoid(g[:, 1])
    return heads(q), heads(k) * (MLSTM_HEAD_DIM ** -0.5), heads(v), ig, lf


def _mlstm_chunk_step(state, inp):
    C, n, m = state
    q, k, v, ig, lf = inp
    L = q.shape[2]
    past_or_self = jnp.tril(jnp.ones((L, L), dtype=bool))
    b = jnp.cumsum(lf, axis=-1)
    log_d = jnp.where(past_or_self, b[..., :, None] - b[..., None, :] + ig[..., None, :], -jnp.inf)
    log_inter = b + m[..., None]
    m_row = jnp.maximum(log_inter, jnp.max(log_d, axis=-1))
    w_intra = jnp.exp(log_d - m_row[..., None])
    w_inter = jnp.exp(log_inter - m_row)
    s = jnp.einsum('bhld,bhsd->bhls', q, k) * w_intra
    num = jnp.einsum('bhls,bhse->bhle', s, v) + w_inter[..., None] * jnp.einsum('bhld,bhde->bhle', q, C)
    den = jnp.sum(s, axis=-1) + w_inter * jnp.einsum('bhld,bhd->bhl', q, n)
    den = jnp.maximum(jnp.abs(den), jnp.exp(-m_row))
    h = num / den[..., None]
    g = b[..., -1]
    log_a = g[..., None] - b + ig
    m_new = jnp.maximum(g + m, jnp.max(log_a, axis=-1))
    w_a = jnp.exp(log_a - m_new[..., None])
    decay = jnp.exp(g + m - m_new)
    C_new = decay[..., None, None] * C + jnp.einsum('bhs,bhsd,bhse->bhde', w_a, k, v)
    n_new = decay[..., None] * n + jnp.einsum('bhs,bhsd->bhd', w_a, k)
    return (C_new, n_new, m_new), h


def _mlstm_chunked(q, k, v, ig, lf, state):
    B, H, S, Dh = q.shape
    nc = S // MLSTM_CHUNK

    def chunks(t):
        return jnp.moveaxis(t.reshape(B, H, nc, MLSTM_CHUNK, *t.shape[3:]), 2, 0)

    state, h = lax.scan(_mlstm_chunk_step, state, (chunks(q), chunks(k), chunks(v), chunks(ig), chunks(lf)))
    return jnp.moveaxis(h, 0, 2).reshape(B, H, S, Dh), state


def _mlstm_bidirectional(ctx_in, lat_in):
    qc, kc, vc, igc, lfc = ctx_in
    qx, kx, vx, igx, lfx = lat_in
    B = qx.shape[0]
    zero = (jnp.zeros((B, MLSTM_HEADS, MLSTM_HEAD_DIM, MLSTM_HEAD_DIM), jnp.float32),
            jnp.zeros((B, MLSTM_HEADS, MLSTM_HEAD_DIM), jnp.float32),
            jnp.zeros((B, MLSTM_HEADS), jnp.float32))

    def flip(t):
        return jnp.flip(t, axis=2)

    hc_f, st_f = _mlstm_chunked(qc, kc, vc, igc[0], lfc[0], zero)
    hx_f, _ = _mlstm_chunked(qx, kx, vx, igx[0], lfx[0], st_f)
    hc_b, st_b = _mlstm_chunked(flip(qc), flip(kc), flip(vc), flip(igc[1]), flip(lfc[1]), zero)
    hx_b, _ = _mlstm_chunked(flip(qx), flip(kx), flip(vx), flip(igx[1]), flip(lfx[1]), st_b)
    return hc_f + flip(hc_b), hx_f + flip(hx_b)


def _mlstm_readout(h, o, norm_w):
    B, H, S, Dh = h.shape
    hn = _layernorm_f32(h).transpose(0, 2, 1, 3).reshape(B, S, H * Dh)
    return (hn * jax.nn.sigmoid(o.astype(jnp.float32))).astype(o.dtype) * norm_w


def _conformer_conv(glu, conv_w, conv_b, ln_w, ln_b):
    a, g = jnp.split(glu, 2, axis=-1)
    u = a * jax.nn.sigmoid(g)
    y = lax.conv_general_dilated(u, conv_w[:, None, :], window_strides=(1,),
                                 padding=[(CONV_KERNEL // 2, CONV_KERNEL // 2)],
                                 dimension_numbers=('NWC', 'WIO', 'NWC'),
                                 feature_group_count=CONV_WIDTH) + conv_b
    y = _layernorm_f32(y).astype(glu.dtype) * ln_w + ln_b
    return jax.nn.silu(y)


def _expert_choice_ffn(xn, w_router, w_gate, w_up, w_down):
    B, n, _ = xn.shape
    cap = EC_CAPACITY_FACTOR * n // N_EXPERTS
    aff = jax.nn.softmax((xn @ w_router).astype(jnp.float32), axis=-1)
    gate, idx = lax.top_k(jnp.swapaxes(aff, 1, 2), cap)
    bidx = jnp.arange(B)[:, None, None]
    xg = jnp.moveaxis(xn[bidx, idx], 1, 0)

    def expert(args):
        xe, wg, wu, wd = args
        return (jax.nn.silu(xe @ wg) * (xe @ wu)) @ wd

    ye = jnp.moveaxis(lax.map(expert, (xg, w_gate, w_up, w_down)), 0, 1)
    ye = ye * gate[..., None].astype(ye.dtype)
    return jnp.zeros_like(xn).at[bidx, idx].add(ye)


def setup_inputs(seed: int = 0) -> dict:
    key = jax.random.key(seed)
    ks = jax.random.split(key, 24)
    f32 = jnp.float32

    def nrm(k, shape, scale):
        return jax.random.normal(k, shape, f32) * scale

    def gain(k, shape):
        return 1.0 + 0.01 * jax.random.normal(k, shape, f32)

    D, L, H, E, F = D_MODEL, DEPTH, MLSTM_HEADS, N_EXPERTS, EXPERT_FF
    f_bias = jnp.linspace(F_BIAS_LO, F_BIAS_HI, H, dtype=f32)
    return {
        'x': nrm(ks[0], (BATCH, SEQ, D), 1.0),
        'c': nrm(ks[1], (BATCH, D), 1.0),
        'ctx': nrm(ks[2], (BATCH, CTX_LEN, D), 1.0),
        'c_ctx': nrm(ks[3], (D,), 1.0),
        'w_ada': nrm(ks[4], (L, D, N_ADA * D), D ** -0.5),
        'b_ada': nrm(ks[5], (L, N_ADA * D), 0.01),
        'norm1_w': gain(ks[6], (L, D)),
        'w_in': nrm(ks[7], (L, D, IN_COLS), D ** -0.5),
        'mlstm_b_i': nrm(ks[8], (L, 2, H), 0.1),
        'mlstm_b_f': f_bias + nrm(ks[9], (L, 2, H), 0.1),
        'mlstm_norm_w': gain(ks[10], (L, MLSTM_WIDTH)),
        'conv_w': nrm(ks[11], (L, CONV_KERNEL, CONV_WIDTH), CONV_KERNEL ** -0.5),
        'conv_b': nrm(ks[12], (L, CONV_WIDTH), 0.01),
        'conv_norm_w': gain(ks[13], (L, CONV_WIDTH)),
        'conv_norm_b': nrm(ks[14], (L, CONV_WIDTH), 0.01),
        'w_out': nrm(ks[15], (L, MIX_WIDTH, D), MIX_WIDTH ** -0.5),
        'norm2_w': gain(ks[16], (L, D)),
        'w_router': nrm(ks[17], (L, D, E), D ** -0.5),
        'w_gate': nrm(ks[18], (L, E, D, F), D ** -0.5),
        'w_up': nrm(ks[19], (L, E, D, F), D ** -0.5),
        'w_down': nrm(ks[20], (L, E, F, D), F ** -0.5),
        'final_norm_w': gain(ks[21], (D,)),
    }


def reference(x, c, ctx, c_ctx, w_ada, b_ada, norm1_w, w_in, mlstm_b_i, mlstm_b_f, mlstm_norm_w,
              conv_w, conv_b, conv_norm_w, conv_norm_b, w_out, norm2_w, w_router, w_gate, w_up,
              w_down, final_norm_w):
    B, S, _ = x.shape
    rows = S // GRID_W
    for l in range(DEPTH):
        last = l == DEPTH - 1
        sh1, sc1, g1, sh2, sc2, g2 = [t[:, None, :] for t in _ada_params(c, w_ada[l], b_ada[l])]
        csh1, csc1, cg1, csh2, csc2, cg2 = _ada_params(c_ctx, w_ada[l], b_ada[l])

        xn = _rmsnorm(x, norm1_w[l]) * (1 + sc1) + sh1
        cn = _rmsnorm(ctx, norm1_w[l]) * (1 + csc1) + csh1
        qx, kx, vx, ox, gx, glux = _split_projection(xn @ w_in[l])
        qc, kc, vc, oc, gc, gluc = _split_projection(cn @ w_in[l])
        h_ctx, h_lat = _mlstm_bidirectional(
            _mlstm_inputs(qc, kc, vc, gc, mlstm_b_i[l], mlstm_b_f[l]),
            _mlstm_inputs(qx, kx, vx, gx, mlstm_b_i[l], mlstm_b_f[l]))
        m_lat = _mlstm_readout(h_lat, ox, mlstm_norm_w[l])
        conv_lat = _conformer_conv(glux.reshape(B * rows, GRID_W, 2 * CONV_WIDTH), conv_w[l], conv_b[l],
                                   conv_norm_w[l], conv_norm_b[l]).reshape(B, S, CONV_WIDTH)
        x = x + g1 * (jnp.concatenate([m_lat, conv_lat], axis=-1) @ w_out[l])

        xn2 = _rmsnorm(x, norm2_w[l]) * (1 + sc2) + sh2
        x = x + g2 * _expert_choice_ffn(xn2, w_router[l], w_gate[l], w_up[l], w_down[l])

        if not last:
            m_ctx = _mlstm_readout(h_ctx, oc, mlstm_norm_w[l])
            conv_ctx = _conformer_conv(gluc, conv_w[l], conv_b[l], conv_norm_w[l], conv_norm_b[l])
            ctx = ctx + cg1 * (jnp.concatenate([m_ctx, conv_ctx], axis=-1) @ w_out[l])
            cn2 = _rmsnorm(ctx, norm2_w[l]) * (1 + csc2) + csh2
            ctx = ctx + cg2 * _expert_choice_ffn(cn2, w_router[l], w_gate[l], w_up[l], w_down[l])
    return _rmsnorm(x, final_norm_w)
```

```python
import functools

import jax
import jax.numpy as jnp
from jax import lax
from jax.experimental import pallas as pl
from jax.experimental.pallas import tpu as pltpu

F32 = jnp.float32
BF16 = jnp.bfloat16

LANES = 128
MXU_DIM = 256
MLSTM_HEADS = 4
MLSTM_CHUNK = 128
GRID_W = 64
EC_CAPACITY_FACTOR = 2
N_ADA = 6
EPS = 1e-6
VMEM_LIMIT_BYTES = 56 * 1024 * 1024


def _cparams(*sem):
    return pltpu.CompilerParams(dimension_semantics=sem, vmem_limit_bytes=VMEM_LIMIT_BYTES)


def _sigmoid(x):
    return 1.0 / (1.0 + jnp.exp(-x))


def _dot(a, b):
    return jnp.dot(a, b, preferred_element_type=F32)


def _dot_nt(a, b):
    return lax.dot_general(a, b, (((1,), (1,)), ((), ())), preferred_element_type=F32)


def _dot_tn(a, b):
    return lax.dot_general(a, b, (((0,), (0,)), ((), ())), preferred_element_type=F32)


def _ada_kernel(c_ref, w_ref, b_ref, o_ref):
    c = c_ref[...]
    a = (c * _sigmoid(c)).astype(BF16)
    o_ref[...] = _dot(a, w_ref[...].astype(BF16)) + b_ref[...]


def _ada(cvec, w_ada, b_ada, tn):
    m, d = cvec.shape
    n = w_ada.shape[1]
    return pl.pallas_call(
        _ada_kernel,
        grid=(n // tn,),
        in_specs=[pl.BlockSpec((m, d), lambda j: (0, 0)),
                  pl.BlockSpec((d, tn), lambda j: (0, j)),
                  pl.BlockSpec((1, tn), lambda j: (0, j))],
        out_specs=pl.BlockSpec((m, tn), lambda j: (0, j)),
        out_shape=jax.ShapeDtypeStruct((m, n), F32),
        compiler_params=_cparams("arbitrary"),
    )(cvec, w_ada, b_ada.reshape(1, n))


def _inproj_kernel(x_ref, sc_ref, sh_ref, nw_ref, w_ref, wgt_ref, bg_ref, p_ref, gt_ref, xn_scr):
    @pl.when(pl.program_id(1) == 0)
    def _():
        x = x_ref[...]
        ms = jnp.mean(x * x, axis=-1, keepdims=True)
        xn = x * lax.rsqrt(ms + EPS) * nw_ref[...]
        xn = xn * (1.0 + sc_ref[0]) + sh_ref[0]
        xb = xn.astype(BF16)
        xn_scr[...] = xb
        gt_ref[...] = _dot_nt(wgt_ref[...], xb) + bg_ref[...]

    p_ref[...] = _dot(xn_scr[...], w_ref[...]).astype(BF16)


def _inproj(x2d, sc, sh, nw, w, wgt, bg, rows_per_group, tm, tn):
    r, d = x2d.shape
    n = w.shape[1]
    ng = wgt.shape[0]
    tiles_per_group = rows_per_group // tm
    return pl.pallas_call(
        _inproj_kernel,
        grid=(r // tm, n // tn),
        in_specs=[pl.BlockSpec((tm, d), lambda i, j: (i, 0)),
                  pl.BlockSpec((1, 1, d), lambda i, j: (i // tiles_per_group, 0, 0)),
                  pl.BlockSpec((1, 1, d), lambda i, j: (i // tiles_per_group, 0, 0)),
                  pl.BlockSpec((1, d), lambda i, j: (0, 0)),
                  pl.BlockSpec((d, tn), lambda i, j: (0, j)),
                  pl.BlockSpec((ng, d), lambda i, j: (0, 0)),
                  pl.BlockSpec((ng, 1), lambda i, j: (0, 0))],
        out_specs=[pl.BlockSpec((tm, tn), lambda i, j: (i, j)),
                   pl.BlockSpec((ng, tm), lambda i, j: (0, i))],
        out_shape=[jax.ShapeDtypeStruct((r, n), BF16),
                   jax.ShapeDtypeStruct((ng, r), F32)],
        scratch_shapes=[pltpu.VMEM((tm, d), BF16)],
        compiler_params=_cparams("arbitrary", "arbitrary"),
    )(x2d, sc, sh, nw, w, wgt, bg)


def _split3(x):
    hi = x.astype(BF16)
    r = x - hi.astype(F32)
    mid = r.astype(BF16)
    lo = (r - mid.astype(F32)).astype(BF16)
    return hi, mid, lo


def _log_sigmoid(z):
    return jnp.minimum(z, 0.0) - jnp.log1p(jnp.exp(-jnp.abs(z)))


def _mlstm_kernel(g_ref, q_ref, k_ref, v_ref, o_ref, kc_ref, vc_ref, nw_ref, out_ref,
                  brow_scr, irow_scr, sf_scr, sb_scr, hf_scr, hb_scr):
    L = MLSTM_CHUNK
    nc = q_ref.shape[1] // L
    ncc = kc_ref.shape[1] // L
    dh = q_ref.shape[2]
    k_scale = dh ** -0.5

    ri = lax.broadcasted_iota(jnp.int32, (L, L), 0)
    ci = lax.broadcasted_iota(jnp.int32, (L, L), 1)
    eye = ri == ci

    def cumsum_lanes(x, tri):
        t = jnp.where(tri, 1.0, 0.0).astype(BF16)
        hi, mid, lo = _split3(x)
        return _dot(hi, t) + _dot(mid, t) + _dot(lo, t)

    g = g_ref[0, :, 0]
    b_fwd = cumsum_lanes(_log_sigmoid(g[1]), ri <= ci)
    b_bwd = cumsum_lanes(_log_sigmoid(g[3]), ri >= ci)
    for r in range(g.shape[1]):
        brow_scr[0, r] = b_fwd[r:r + 1, :]
        brow_scr[1, r] = b_bwd[r:r + 1, :]
        irow_scr[0, r] = g[0, r:r + 1, :]
        irow_scr[1, r] = g[2, r:r + 1, :]

    ones_blk = jnp.ones((L, LANES), BF16)

    def step(d, row, q, k, v, s_scr, m, h_out):
        brow = brow_scr[d, row]
        igrow = irow_scr[d, row]
        bcol = jnp.sum(jnp.where(eye, brow, 0.0), axis=1, keepdims=True)
        igcol = jnp.sum(jnp.where(eye, igrow, 0.0), axis=1, keepdims=True)
        gtot = brow[:, L - 1:L] if d == 0 else brow[:, 0:1]
        k = k * k_scale
        v_aug = jnp.concatenate([v, ones_blk], axis=1)
        s_old = s_scr[...]
        if h_out is not None:
            mask = (ci <= ri) if d == 0 else (ci >= ri)
            log_d = jnp.where(mask, bcol - brow + igrow, -jnp.inf)
            log_inter = bcol + m
            m_row = jnp.maximum(log_inter, jnp.max(log_d, axis=1, keepdims=True))
            w_intra = jnp.exp(log_d - m_row)
            w_inter = jnp.exp(log_inter - m_row)
            s = (_dot_nt(q, k) * w_intra).astype(BF16)
            tot = _dot(s, v_aug) + w_inter * _dot(q, s_old.astype(BF16))
            den = jnp.maximum(jnp.abs(tot[:, dh:dh + 1]), jnp.exp(-m_row))
            h_ref, h_row = h_out
            h_ref[pl.ds(h_row, L), :] = tot[:, :dh] / den
        log_a = gtot - bcol + igcol
        m_new = jnp.maximum(gtot + m, jnp.max(log_a, axis=0, keepdims=True))
        w_a = jnp.exp(log_a - m_new)
        decay = jnp.exp(gtot + m - m_new)
        vs = (v_aug.astype(F32) * w_a).astype(BF16)
        s_scr[...] = decay * s_old + _dot_tn(k, vs)
        return m_new

    sf_scr[...] = jnp.zeros_like(sf_scr)
    sb_scr[...] = jnp.zeros_like(sb_scr)
    m_f = jnp.zeros((1, 1), F32)
    m_b = jnp.zeros((1, 1), F32)
    for c in range(ncc):
        m_f = step(0, c, None, kc_ref[0, c * L:(c + 1) * L, :], vc_ref[0, c * L:(c + 1) * L, :],
                   sf_scr, m_f, None)
    for c in reversed(range(ncc)):
        m_b = step(1, c, None, kc_ref[0, c * L:(c + 1) * L, :], vc_ref[0, c * L:(c + 1) * L, :],
                   sb_scr, m_b, None)

    def body(t, carry):
        m_f, m_b = carry
        rf = pl.multiple_of(t * L, L)
        rb = pl.multiple_of((nc - 1 - t) * L, L)
        m_f = step(0, ncc + t, q_ref[0, pl.ds(rf, L), :], k_ref[0, pl.ds(rf, L), :],
                   v_ref[0, pl.ds(rf, L), :], sf_scr, m_f, (hf_scr, rf))
        m_b = step(1, ncc + nc - 1 - t, q_ref[0, pl.ds(rb, L), :], k_ref[0, pl.ds(rb, L), :],
                   v_ref[0, pl.ds(rb, L), :], sb_scr, m_b, (hb_scr, rb))
        return m_f, m_b

    lax.fori_loop(0, nc, body, (m_f, m_b))

    def readout(t, carry):
        r0 = pl.multiple_of(t * L, L)
        h = hf_scr[pl.ds(r0, L), :] + hb_scr[pl.ds(r0, L), :]
        mu = jnp.mean(h, axis=-1, keepdims=True)
        hc = h - mu
        var = jnp.mean(hc * hc, axis=-1, keepdims=True)
        hn = hc * lax.rsqrt(var + EPS)
        og = _sigmoid(o_ref[0, pl.ds(r0, L), :].astype(F32))
        out_ref[0, pl.ds(r0, L), :] = ((hn * og) * nw_ref[...]).astype(out_ref.dtype)
        return carry

    lax.fori_loop(0, nc, readout, 0)


def _mlstm(gates, p3, pc3, norm_w):
    n_heads, _, b, ncp, L = gates.shape
    s = p3.shape[1]
    sc = pc3.shape[1]
    w = norm_w.shape[1]
    dh = w // n_heads
    blk = lambda off: pl.BlockSpec((1, s, dh), lambda bi, hi: (bi, 0, off + hi))
    blkc = lambda off: pl.BlockSpec((1, sc, dh), lambda bi, hi: (bi, 0, off + hi))
    return pl.pallas_call(
        _mlstm_kernel,
        grid=(b, n_heads),
        in_specs=[pl.BlockSpec((1, 4, 1, ncp, L), lambda bi, hi: (hi, 0, bi, 0, 0)),
                  blk(0), blk(n_heads), blk(2 * n_heads), blk(3 * n_heads),
                  blkc(0), blkc(n_heads),
                  pl.BlockSpec((1, dh), lambda bi, hi: (0, hi))],
        out_specs=pl.BlockSpec((1, s, dh), lambda bi, hi: (bi, 0, hi)),
        out_shape=jax.ShapeDtypeStruct((b, s, w), BF16),
        scratch_shapes=[pltpu.VMEM((2, ncp, 1, L), F32), pltpu.VMEM((2, ncp, 1, L), F32),
                        pltpu.VMEM((dh, dh + LANES), F32), pltpu.VMEM((dh, dh + LANES), F32),
                        pltpu.VMEM((s, dh), F32), pltpu.VMEM((s, dh), F32)],
        compiler_params=_cparams("arbitrary", "arbitrary"),
    )(gates, p3, p3, p3, p3, pc3, pc3, norm_w)


def _conv_kernel(a_ref, g_ref, cw_ref, cb_ref, lw_ref, lb_ref, o_ref, pad_scr, y_scr):
    tm, cwid = a_ref.shape
    kw = cw_ref.shape[0]
    half = kw // 2
    top = pad_scr.shape[0] - GRID_W - half - 1
    pad_scr[...] = jnp.zeros_like(pad_scr)

    def group(r, carry):
        r0 = pl.multiple_of(r * GRID_W, GRID_W)
        a = a_ref[pl.ds(r0, GRID_W), :].astype(F32)
        gg = g_ref[pl.ds(r0, GRID_W), :].astype(F32)
        pad_scr[top:top + GRID_W, :] = a * _sigmoid(gg)
        for cb in range(cwid // LANES):
            cs = slice(cb * LANES, (cb + 1) * LANES)
            acc = jnp.zeros((GRID_W, LANES), F32) + cb_ref[:, cs]
            for j in range(kw):
                off = top - half + j
                acc = acc + cw_ref[j:j + 1, cs] * pad_scr[off:off + GRID_W, cs]
            y_scr[:, cs] = acc
        y = y_scr[...]
        mu = jnp.mean(y, axis=-1, keepdims=True)
        yc = y - mu
        var = jnp.mean(yc * yc, axis=-1, keepdims=True)
        z = (yc * lax.rsqrt(var + EPS)) * lw_ref[...] + lb_ref[...]
        o_ref[pl.ds(r0, GRID_W), :] = (z * _sigmoid(z)).astype(o_ref.dtype)
        return carry

    lax.fori_loop(0, tm // GRID_W, group, 0)


def _conv(p2d, col_blk, conv_w, conv_b, ln_w, ln_b, tm):
    r = p2d.shape[0]
    kw, cwid = conv_w.shape
    pad_rows = 16 + GRID_W + 16
    vec = lambda: pl.BlockSpec((1, cwid), lambda i: (0, 0))
    return pl.pallas_call(
        _conv_kernel,
        grid=(r // tm,),
        in_specs=[pl.BlockSpec((tm, cwid), lambda i: (i, col_blk)),
                  pl.BlockSpec((tm, cwid), lambda i: (i, col_blk + 1)),
                  pl.BlockSpec((kw, cwid), lambda i: (0, 0)),
                  vec(), vec(), vec()],
        out_specs=pl.BlockSpec((tm, cwid), lambda i: (i, 0)),
        out_shape=jax.ShapeDtypeStruct((r, cwid), BF16),
        scratch_shapes=[pltpu.VMEM((pad_rows, cwid), F32), pltpu.VMEM((GRID_W, cwid), F32)],
        compiler_params=_cparams("arbitrary"),
    )(p2d, p2d, conv_w, conv_b.reshape(1, cwid), ln_w.reshape(1, cwid), ln_b.reshape(1, cwid))


def _outproj_kernel(m_ref, c_ref, x_ref, w1_ref, w2_ref, g1_ref, nw_ref, sc_ref, sh_ref, wr_ref,
                    x1_ref, xn_ref, lg_ref):
    acc = _dot(m_ref[...], w1_ref[...]) + _dot(c_ref[...], w2_ref[...])
    x1 = x_ref[...] + g1_ref[0] * acc
    x1_ref[...] = x1
    ms = jnp.mean(x1 * x1, axis=-1, keepdims=True)
    xn = x1 * lax.rsqrt(ms + EPS) * nw_ref[...]
    xn = xn * (1.0 + sc_ref[0]) + sh_ref[0]
    xn_ref[...] = xn
    lg_ref[...] = _dot(xn.astype(BF16), wr_ref[...])


def _outproj(mlat, conv, x2d, w1, w2, g1, nw, sc, sh, wr, rows_per_group, tm):
    r, d = x2d.shape
    wm = mlat.shape[1]
    wc = conv.shape[1]
    tpg = rows_per_group // tm
    full = lambda a: pl.BlockSpec(a.shape, lambda i: (0,) * a.ndim)
    per_b = lambda: pl.BlockSpec((1, 1, d), lambda i: (i // tpg, 0, 0))
    return pl.pallas_call(
        _outproj_kernel,
        grid=(r // tm,),
        in_specs=[pl.BlockSpec((tm, wm), lambda i: (i, 0)),
                  pl.BlockSpec((tm, wc), lambda i: (i, 0)),
                  pl.BlockSpec((tm, d), lambda i: (i, 0)),
                  full(w1), full(w2), per_b(), full(nw), per_b(), per_b(), full(wr)],
        out_specs=[pl.BlockSpec((tm, d), lambda i: (i, 0)),
                   pl.BlockSpec((tm, d), lambda i: (i, 0)),
                   pl.BlockSpec((tm, LANES), lambda i: (i, 0))],
        out_shape=[jax.ShapeDtypeStruct((r, d), F32),
                   jax.ShapeDtypeStruct((r, d), F32),
                   jax.ShapeDtypeStruct((r, LANES), F32)],
        compiler_params=_cparams("arbitrary"),
    )(mlat, conv, x2d, w1, w2, g1, nw, sc, sh, wr)


def _route_kernel(lg_ref, slot_ref, idx_ref, cs_scr, *, cap):
    n_exp, s = lg_ref.shape[1], lg_ref.shape[2]
    z = lg_ref[0]
    ez = jnp.exp(z - jnp.max(z, axis=0, keepdims=True))
    aff = ez / jnp.sum(ez, axis=0, keepdims=True)
    bits = lax.bitcast_convert_type(aff, jnp.int32)

    def search(i, thr):
        cand = thr | (jnp.int32(1) << (30 - i))
        cnt = jnp.sum(jnp.where(bits >= cand, 1.0, 0.0), axis=1, keepdims=True)
        return jnp.where(cnt >= cap, cand, thr)

    thr = lax.fori_loop(0, 31, search, jnp.zeros((n_exp, 1), jnp.int32))

    ri = lax.broadcasted_iota(jnp.int32, (LANES, LANES), 0)
    ci = lax.broadcasted_iota(jnp.int32, (LANES, LANES), 1)
    tri = jnp.where(ri <= ci, 1.0, 0.0).astype(BF16)

    def cumsum_lanes(x):
        outs = []
        carry = jnp.zeros((n_exp, 1), F32)
        for blk in range(s // LANES):
            c = _dot(x[:, blk * LANES:(blk + 1) * LANES].astype(BF16), tri) + carry
            outs.append(c)
            carry = c[:, LANES - 1:LANES]
        return jnp.concatenate(outs, axis=1)

    gt = jnp.where(bits > thr, 1.0, 0.0)
    eq = jnp.where(bits == thr, 1.0, 0.0)
    need = cap - jnp.sum(gt, axis=1, keepdims=True)
    tie_rank = cumsum_lanes(eq) - eq
    mask = jnp.maximum(gt, eq * jnp.where(tie_rank < need, 1.0, 0.0))
    cs = cumsum_lanes(mask)
    slot_ref[0] = jnp.where(mask > 0.0, cs - 1.0, -1.0).astype(jnp.int32)
    for e in range(n_exp):
        cs_scr[e] = cs[e:e + 1, :]

    p_col = lax.broadcasted_iota(jnp.int32, (cap, 1), 0).astype(F32)

    def token_of_slot(e, carry):
        acc = jnp.zeros((cap, LANES), F32)
        for blk in range(s // LANES):
            row = cs_scr[e, :, blk * LANES:(blk + 1) * LANES]
            acc = acc + jnp.where(row <= p_col, 1.0, 0.0)
        tok = jnp.sum(acc, axis=1, keepdims=True)
        idx_ref[0, pl.ds(e, 1)] = jnp.broadcast_to(tok, (1, cap, LANES)).astype(jnp.int32)
        return carry

    lax.fori_loop(0, n_exp, token_of_slot, 0)


def _route(logits_t, cap):
    b, n_exp, s = logits_t.shape
    return pl.pallas_call(
        functools.partial(_route_kernel, cap=cap),
        grid=(b,),
        in_specs=[pl.BlockSpec((1, n_exp, s), lambda i: (i, 0, 0))],
        out_specs=[pl.BlockSpec((1, n_exp, s), lambda i: (i, 0, 0)),
                   pl.BlockSpec((1, n_exp, cap, LANES), lambda i: (i, 0, 0, 0))],
        out_shape=[jax.ShapeDtypeStruct((b, n_exp, s), jnp.int32),
                   jax.ShapeDtypeStruct((b, n_exp, cap, LANES), jnp.int32)],
        scratch_shapes=[pltpu.VMEM((n_exp, 1, s), F32)],
        compiler_params=_cparams("arbitrary"),
    )(logits_t)


def _gather_kernel(rows_ref, src_ref, o_ref, buf, sem):
    cap = o_ref.shape[0]
    base = pl.program_id(0) * cap

    def row_copy(p):
        return pltpu.make_async_copy(src_ref.at[pl.ds(rows_ref[base + p], 1)], buf.at[pl.ds(p, 1)], sem)

    def start(p, carry):
        row_copy(p).start()
        return carry

    def wait(p, carry):
        row_copy(p).wait()
        return carry

    lax.fori_loop(0, cap, start, 0)
    lax.fori_loop(0, cap, wait, 0)
    o_ref[...] = buf[...].astype(o_ref.dtype)


def _gather(rows, src, cap):
    n = rows.shape[0]
    d = src.shape[1]
    return pl.pallas_call(
        _gather_kernel,
        grid_spec=pltpu.PrefetchScalarGridSpec(
            num_scalar_prefetch=1,
            grid=(n // cap,),
            in_specs=[pl.BlockSpec(memory_space=pl.ANY)],
            out_specs=pl.BlockSpec((cap, d), lambda g, rows_ref: (g, 0)),
            scratch_shapes=[pltpu.VMEM((cap, d), F32), pltpu.SemaphoreType.DMA(())]),
        out_shape=jax.ShapeDtypeStruct((n, d), BF16),
        compiler_params=_cparams("arbitrary"),
    )(rows, src)


def _ffn_kernel(x_ref, wg_ref, wu_ref, wd_ref, wgt_ref, wut_ref, wdt_ref, o_ref, *, n_main, has_tail):
    j = pl.program_id(2)

    def contrib(wg, wu, wd):
        x = x_ref[...]
        g = _dot(x, wg[0].astype(BF16))
        u = _dot(x, wu[0].astype(BF16))
        h = ((g * _sigmoid(g)) * u).astype(BF16)
        return _dot(h, wd[0].astype(BF16))

    @pl.when(j == 0)
    def _():
        o_ref[...] = contrib(wg_ref, wu_ref, wd_ref)

    @pl.when(jnp.logical_and(j > 0, j < n_main))
    def _():
        o_ref[...] += contrib(wg_ref, wu_ref, wd_ref)

    if has_tail:
        @pl.when(j == n_main)
        def _():
            o_ref[...] += contrib(wgt_ref, wut_ref, wdt_ref)


def _ffn(xg, w_gate, w_up, w_down, rows_per_expert, tm, tf):
    n, d = xg.shape
    n_exp, _, f = w_gate.shape
    n_main = f // tf
    tail = f - n_main * tf
    has_tail = tail > 0
    tail_w = tail if has_tail else LANES
    tail_blk = (n_main * tf) // tail_w
    mt = rows_per_expert // tm
    nj = n_main + (1 if has_tail else 0)
    jm = lambda j: jnp.minimum(j, n_main - 1)
    return pl.pallas_call(
        functools.partial(_ffn_kernel, n_main=n_main, has_tail=has_tail),
        grid=(n_exp, mt, nj),
        in_specs=[pl.BlockSpec((tm, d), lambda e, m, j: (e * mt + m, 0)),
                  pl.BlockSpec((1, d, tf), lambda e, m, j: (e, 0, jm(j))),
                  pl.BlockSpec((1, d, tf), lambda e, m, j: (e, 0, jm(j))),
                  pl.BlockSpec((1, tf, d), lambda e, m, j: (e, jm(j), 0)),
                  pl.BlockSpec((1, d, tail_w), lambda e, m, j: (e, 0, tail_blk)),
                  pl.BlockSpec((1, d, tail_w), lambda e, m, j: (e, 0, tail_blk)),
                  pl.BlockSpec((1, tail_w, d), lambda e, m, j: (e, tail_blk, 0))],
        out_specs=pl.BlockSpec((tm, d), lambda e, m, j: (e * mt + m, 0)),
        out_shape=jax.ShapeDtypeStruct((n, d), F32),
        compiler_params=_cparams("arbitrary", "arbitrary", "arbitrary"),
    )(xg, w_gate, w_up, w_down, w_gate, w_up, w_down)


def _combine_kernel(ye_ref, slot_ref, lg_ref, x1_ref, g2_ref, fw_ref, o_ref, acc_scr, *, n_exp):
    e = pl.program_id(2)
    tt = acc_scr.shape[0]
    cap = ye_ref.shape[0]

    @pl.when(e == 0)
    def _():
        acc_scr[...] = jnp.zeros_like(acc_scr)

    lane = lax.broadcasted_iota(jnp.int32, (tt, LANES), 1)
    z = jnp.where(lane < n_exp, lg_ref[...], -jnp.inf)
    ez = jnp.exp(z - jnp.max(z, axis=-1, keepdims=True))
    aff = ez / jnp.sum(ez, axis=-1, keepdims=True)
    gate = jnp.sum(jnp.where(lane == e, aff, 0.0), axis=-1, keepdims=True)
    lane_e = lax.broadcasted_iota(jnp.int32, (tt, n_exp), 1)
    slot = jnp.sum(jnp.where(lane_e == e, slot_ref[0].astype(F32), 0.0), axis=-1, keepdims=True)
    p_row = lax.broadcasted_iota(jnp.int32, (1, cap), 1).astype(F32)
    onehot = jnp.where(slot == p_row, 1.0, 0.0).astype(BF16)
    acc_scr[...] += gate * _dot(onehot, ye_ref[...].astype(BF16))

    @pl.when(e == n_exp - 1)
    def _():
        y = x1_ref[...] + g2_ref[0] * acc_scr[...]
        ms = jnp.mean(y * y, axis=-1, keepdims=True)
        o_ref[...] = y * lax.rsqrt(ms + EPS) * fw_ref[...]


def _combine(ye, slot_t, logits, x1, g2, fw, cap, tt):
    b, s, n_exp = slot_t.shape
    r, d = x1.shape
    tiles = s // tt
    return pl.pallas_call(
        functools.partial(_combine_kernel, n_exp=n_exp),
        grid=(b, tiles, n_exp),
        in_specs=[pl.BlockSpec((cap, d), lambda bi, i, e: (e * b + bi, 0)),
                  pl.BlockSpec((1, tt, n_exp), lambda bi, i, e: (bi, i, 0)),
                  pl.BlockSpec((tt, LANES), lambda bi, i, e: (bi * tiles + i, 0)),
                  pl.BlockSpec((tt, d), lambda bi, i, e: (bi * tiles + i, 0)),
                  pl.BlockSpec((1, 1, d), lambda bi, i, e: (bi, 0, 0)),
                  pl.BlockSpec((1, d), lambda bi, i, e: (0, 0))],
        out_specs=pl.BlockSpec((tt, d), lambda bi, i, e: (bi * tiles + i, 0)),
        out_shape=jax.ShapeDtypeStruct((r, d), F32),
        scratch_shapes=[pltpu.VMEM((tt, d), F32)],
        compiler_params=_cparams("arbitrary", "arbitrary", "arbitrary"),
    )(ye, slot_t, logits, x1, g2, fw)


def _tile(dim, candidates):
    for c in candidates:
        if dim % c == 0:
            return c
    return dim


def kernel(x, c, ctx, c_ctx, w_ada, b_ada, norm1_w, w_in, mlstm_b_i, mlstm_b_f, mlstm_norm_w,
           conv_w, conv_b, conv_norm_w, conv_norm_b, w_out, norm2_w, w_router, w_gate, w_up,
           w_down, final_norm_w):
    b, s, d = x.shape
    sc_len = ctx.shape[1]
    depth = w_ada.shape[0]
    assert depth == 1, "the context-stream update between layers is not implemented"
    n_heads = MLSTM_HEADS
    wm = mlstm_norm_w.shape[1]
    dh = wm // n_heads
    wc = conv_w.shape[2]
    n_exp = w_router.shape[2]
    cap = EC_CAPACITY_FACTOR * s // n_exp
    L = MLSTM_CHUNK
    n_gates = 4 * n_heads
    assert w_in.shape[2] == 4 * wm + n_gates + 2 * wc and wm == wc
    assert s % L == 0 and sc_len % L == 0 and s % GRID_W == 0 and dh % LANES == 0

    cvec = jnp.concatenate([c, c_ctx[None, :], jnp.zeros((-(b + 1) % 8, d), F32)], axis=0)
    ada = _ada(cvec, w_ada[0], b_ada[0], _tile(N_ADA * d, (1024, 512, 256, 128)))
    sh1, sc1, g1, sh2, sc2, g2 = [ada[:, i * d:(i + 1) * d][:, None, :] for i in range(N_ADA)]

    w_in0 = w_in[0]
    w_main = jnp.concatenate([w_in0[:, :4 * wm], w_in0[:, 4 * wm + n_gates:]], axis=1).astype(BF16)
    w_kv = w_main[:, wm:3 * wm]
    perm = jnp.asarray([dr * 2 * n_heads + gt * n_heads + h
                        for h in range(n_heads) for dr in range(2) for gt in range(2)])
    w_gt = w_in0[:, 4 * wm:4 * wm + n_gates][:, perm].T.astype(BF16)
    b_gates = jnp.stack([mlstm_b_i[0], mlstm_b_f[0]], axis=1).reshape(-1)[perm].reshape(n_gates, 1)
    nw1 = norm1_w[0].reshape(1, d)

    x2d = x.reshape(b * s, d)
    tm_in = _tile(s, (1024, 512, 256, 128))
    p, gt_x = _inproj(x2d, sc1[:b], sh1[:b], nw1, w_main, w_gt, b_gates, s,
                      tm_in, _tile(w_main.shape[1], (1024, 512, 256, 128)))
    tm_c = _tile(sc_len, (1024, 512, 256, 128))
    pc, gt_c = _inproj(ctx.reshape(b * sc_len, d), sc1[b:b + 1], sh1[b:b + 1], nw1, w_kv, w_gt, b_gates,
                       b * sc_len, tm_c, _tile(w_kv.shape[1], (1024, 512, 256, 128)))

    nc, ncc = s // L, sc_len // L
    ncp = -(-(nc + ncc) // 8) * 8
    gates = jnp.concatenate([gt_c.reshape(n_heads, 4, b, ncc, L), gt_x.reshape(n_heads, 4, b, nc, L),
                             jnp.zeros((n_heads, 4, b, ncp - nc - ncc, L), F32)], axis=3)
    mlat = _mlstm(gates, p.reshape(b, s, -1), pc.reshape(b, sc_len, -1), mlstm_norm_w[0].reshape(1, wm))

    conv = _conv(p, (4 * wm) // wc, conv_w[0], conv_b[0], conv_norm_w[0], conv_norm_b[0],
                 _tile(s, (512, 256, 128, 64)))

    w_out0 = w_out[0].astype(BF16)
    w_r = jnp.zeros((d, LANES), BF16).at[:, :n_exp].set(w_router[0].astype(BF16))
    x1, xn2, logits = _outproj(mlat.reshape(b * s, wm), conv, x2d, w_out0[:wm], w_out0[wm:], g1[:b],
                               norm2_w[0].reshape(1, d), sc2[:b], sh2[:b], w_r, s,
                               _tile(s, (512, 256, 128)))

    logits_t = logits[:, :n_exp].reshape(b, s, n_exp).transpose(0, 2, 1)
    slot, idx = _route(logits_t, cap)
    rows = (idx[..., 0] + (jnp.arange(b, dtype=jnp.int32) * s)[:, None, None]).transpose(1, 0, 2).reshape(-1)
    xg = _gather(rows, xn2, cap)
    ye = _ffn(xg, w_gate[0], w_up[0], w_down[0], b * cap, _tile(b * cap, (1024, 512, 256, 128)), MXU_DIM)
    out = _combine(ye, slot.transpose(0, 2, 1), logits, x1, g2[:b], final_norm_w.reshape(1, d), cap,
                   _tile(s, (512, 256, 128)))
    return out.reshape(b, s, d)
```
